```python
import math
import jax, jax.numpy as jnp
from jax import lax
import numpy as np

D_MODEL = 2048
BATCH = 4
SEQ = 2048
DEPTH = 1
DEC_BATCH = 128
DEC_SEQ = 8
PAST_LEN = 16384
PAGE_SIZE = 128

D_A = D_MODEL // 2
HEAD_SIZE = 64
N_HEADS_A = D_A // HEAD_SIZE
LORA_W = max(32, int(round(math.sqrt(D_MODEL) * 1.8 / 32)) * 32)
LORA_A = max(32, int(round(math.sqrt(D_MODEL) * 1.8 / 32)) * 32)
LORA_G = max(32, int(round(D_MODEL ** 0.8 * 0.6 / 32)) * 32)
GN_EPS = 64e-5
D_B = D_MODEL - D_A
POOL_WINDOWS = (2, 4, 8, 16)
N_POOL_GROUPS = len(POOL_WINDOWS)
POOL_GROUP = D_B // N_POOL_GROUPS
POOL_BUF = max(POOL_WINDOWS) - 1
N_SHIFT = 3 * D_A + LORA_W + LORA_A + LORA_G
N_IN = N_SHIFT + D_B + 2 * D_MODEL
N_EXPERTS = 32
TOP_K = 4
D_FF = D_MODEL
SWIGLU_ALPHA = 1.702
SWIGLU_LIMIT = 7.0
MOE_BLOCK = 128
D_PLE = 256
RMS_EPS = 1e-6

kernel_name = 'rwkv7_pool_moe_hybrid_step'


def rms_norm(x, g):
    xf = x.astype(jnp.float32)
    return xf * lax.rsqrt(jnp.mean(xf * xf, axis=-1, keepdims=True) + RMS_EPS) * g


def wkv7_scan(r, w, k, v, a, b, s0):
    def step(s, inp):
        r_t, w_t, k_t, v_t, a_t, b_t = inp
        sa = jnp.einsum('bhvk,bhk->bhv', s, a_t)
        s = s * w_t[:, :, None, :] + sa[..., None] * b_t[:, :, None, :] + v_t[..., None] * k_t[:, :, None, :]
        return s, jnp.einsum('bhvk,bhk->bhv', s, r_t)
    xs = (jnp.moveaxis(r, 1, 0), jnp.moveaxis(w, 1, 0), jnp.moveaxis(k, 1, 0),
          jnp.moveaxis(v, 1, 0), jnp.moveaxis(a, 1, 0), jnp.moveaxis(b, 1, 0))
    s, ys = lax.scan(step, s0, xs)
    return jnp.moveaxis(ys, 0, 1), s


def rwkv7_mix(cur, prev, s0, tm_mix, w0, w2, a0, a2, g2, k_k, k_a, r_k, ln_x_w, ln_x_b):
    f32 = jnp.float32
    cur = cur.astype(f32)
    m = cur + (prev.astype(f32) - cur) * tm_mix
    o1, o2, o3 = D_A, 2 * D_A, 3 * D_A
    o4, o5 = o3 + LORA_W, o3 + LORA_W + LORA_A
    r, k, v = m[..., :o1], m[..., o1:o2], m[..., o2:o3]
    xw, xa, xg = m[..., o3:o4], m[..., o4:o5], m[..., o5:N_SHIFT]
    w_log = -jax.nn.softplus(-(w0 + jnp.tanh(xw) @ w2)) - 0.5
    decay = jnp.exp(-jnp.exp(w_log))
    a = jax.nn.sigmoid(a0 + xa @ a2)
    g = jax.nn.sigmoid(xg) @ g2
    bn, seq_len, _ = m.shape
    def heads(t):
        return t.reshape(bn, seq_len, N_HEADS_A, HEAD_SIZE)
    kk = heads(k * k_k)
    kk = kk / jnp.maximum(jnp.sqrt(jnp.sum(kk * kk, axis=-1, keepdims=True)), 1e-12)
    k = k * (1.0 + (a - 1.0) * k_a)
    r_h, k_h, v_h, a_h, w_h = heads(r), heads(k), heads(v), heads(a), heads(decay)
    y, s_new = wkv7_scan(r_h, w_h, k_h, v_h, -kk, kk * a_h, s0.astype(f32))
    mu = jnp.mean(y, axis=-1, keepdims=True)
    var = jnp.mean(jnp.square(y - mu), axis=-1, keepdims=True)
    y = ((y - mu) * lax.rsqrt(var + GN_EPS)).reshape(bn, seq_len, D_A) * ln_x_w + ln_x_b
    bonus = (jnp.sum(r_h * k_h * r_k, axis=-1, keepdims=True) * v_h).reshape(bn, seq_len, D_A)
    return (y + bonus) * g, s_new


def pool_mix(u, u_prev, pos0, w_pool, pool_scale):
    bn, seq_len, _ = u.shape
    ext = jnp.concatenate([u_prev.astype(jnp.float32), u.astype(jnp.float32)], axis=1)
    cs = jnp.concatenate([jnp.zeros((bn, 1, D_B), jnp.float32), jnp.cumsum(ext, axis=1)], axis=1)
    pos = pos0 + jnp.arange(seq_len, dtype=jnp.int32)
    means = []
    for j, win in enumerate(POOL_WINDOWS):
        c0, c1 = j * POOL_GROUP, (j + 1) * POOL_GROUP
        upper = cs[:, POOL_BUF + 1:POOL_BUF + 1 + seq_len, c0:c1]
        lower = cs[:, POOL_BUF + 1 - win:POOL_BUF + 1 - win + seq_len, c0:c1]
        cnt = jnp.minimum(pos + 1, win).astype(jnp.float32)[None, :, None]
        means.append((upper - lower) / cnt)
    pooled = jnp.concatenate(means, axis=-1) - ext[:, POOL_BUF:]
    mixed = jnp.einsum('blgc,gcd->blgd', pooled.reshape(bn, seq_len, N_POOL_GROUPS, POOL_GROUP), w_pool)
    return mixed.reshape(bn, seq_len, D_B) * pool_scale, ext[:, -POOL_BUF:]


def moe(h, w_router, b_router, w_exp_in, b_exp_in, w_exp_out, b_exp_out):
    bn, seq_len, d = h.shape
    n_tok = bn * seq_len
    xt = h.reshape(n_tok, d)
    logits = (xt @ w_router).astype(jnp.float32) + b_router
    top_val, top_idx = lax.top_k(logits, TOP_K)
    gates = jax.nn.softmax(top_val, axis=-1)
    n_assign = n_tok * TOP_K
    e_flat = top_idx.reshape(-1)
    order = jnp.argsort(e_flat)
    e_sorted = e_flat[order]
    tok_sorted = order // TOP_K
    gate_sorted = gates.reshape(-1)[order]
    counts = jnp.zeros((N_EXPERTS,), jnp.int32).at[e_flat].add(1)
    padded = (counts + MOE_BLOCK - 1) // MOE_BLOCK * MOE_BLOCK
    pad_end = jnp.cumsum(padded)
    pad_start = pad_end - padded
    start = jnp.cumsum(counts) - counts
    dest = pad_start[e_sorted] + (jnp.arange(n_assign, dtype=jnp.int32) - start[e_sorted])
    n_blocks = -(-n_assign // MOE_BLOCK) + N_EXPERTS
    rows = jnp.zeros((n_blocks * MOE_BLOCK, d), xt.dtype).at[dest].set(xt[tok_sorted])
    block_off = jnp.arange(n_blocks, dtype=jnp.int32) * MOE_BLOCK
    block_expert = jnp.minimum(jnp.searchsorted(pad_end, block_off, side='right'), N_EXPERTS - 1)

    def expert_block(args):
        xb, e = args
        hid = xb @ w_exp_in[e] + b_exp_in[e]
        gate = jnp.minimum(hid[:, :D_FF], SWIGLU_LIMIT)
        up = jnp.clip(hid[:, D_FF:], -SWIGLU_LIMIT, SWIGLU_LIMIT)
        glu = gate * jax.nn.sigmoid(gate * SWIGLU_ALPHA)
        return ((up + 1.0) * glu) @ w_exp_out[e] + b_exp_out[e]

    out = lax.map(expert_block, (rows.reshape(n_blocks, MOE_BLOCK, d), block_expert))
    y_assign = out.reshape(n_blocks * MOE_BLOCK, d)[dest] * gate_sorted[:, None]
    y = jax.ops.segment_sum(y_assign, tok_sorted, num_segments=n_tok)
    return y.reshape(bn, seq_len, d)


def decoder_layer(x, p, s_wkv, h_prev, u_prev, pos0,
                  norm_mix, w_in, tm_mix, w0, w2, a0, a2, g2, k_k, k_a, r_k, ln_x_w, ln_x_b,
                  w_pool, pool_scale, w_out_a, w_out_b, w_out, norm_ffn,
                  w_router, b_router, w_exp_in, b_exp_in, w_exp_out, b_exp_out,
                  norm_ple, w_ple_gate, w_ple_proj):
    h = rms_norm(x, norm_mix)
    proj = h @ w_in
    cur = proj[..., :N_SHIFT]
    first_prev = (h_prev.astype(jnp.float32) @ w_in[:, :N_SHIFT])[:, None]
    prev = jnp.concatenate([first_prev.astype(cur.dtype), cur[:, :-1]], axis=1)
    o_a, s_new = rwkv7_mix(cur, prev, s_wkv, tm_mix, w0, w2, a0, a2, g2, k_k, k_a, r_k, ln_x_w, ln_x_b)
    u = proj[..., N_SHIFT:N_SHIFT + D_B]
    o_b, pool_new = pool_mix(u, u_prev, pos0, w_pool, pool_scale)
    g0 = N_SHIFT + D_B
    gate_a = jax.nn.sigmoid(proj[..., g0:g0 + D_MODEL])
    gate_b = jax.nn.sigmoid(proj[..., g0 + D_MODEL:])
    merged = gate_a * (o_a @ w_out_a) + gate_b * (o_b @ w_out_b)
    x = x + merged @ w_out
    x = x + moe(rms_norm(x, norm_ffn), w_router, b_router, w_exp_in, b_exp_in, w_exp_out, b_exp_out)
    hp = rms_norm(x, norm_ple)
    x = x + jax.nn.sigmoid(hp @ w_ple_gate) * (p @ w_ple_proj)
    return x, s_new, h[:, -1], pool_new


def setup_inputs(seed: int = 0) -> dict:
    key = jax.random.key(seed)
    keys = iter(jax.random.split(key, 48))
    f32 = jnp.float32

    def nrm(shape, scale):
        return jax.random.normal(next(keys), shape, f32) * scale

    def gain(shape):
        return 1.0 + nrm(shape, 0.1)

    def unif(shape, lo, hi):
        return jax.random.uniform(next(keys), shape, f32, lo, hi)

    return {
        'x_prompt': nrm((BATCH, SEQ, D_MODEL), 1.0),
        'x_sample': nrm((DEC_BATCH, DEC_SEQ, D_MODEL), 1.0),
        'state_wkv': nrm((DEPTH, DEC_BATCH, N_HEADS_A, HEAD_SIZE, HEAD_SIZE), 0.3),
        'state_shift': nrm((DEPTH, DEC_BATCH, D_MODEL), 1.0),
        'state_pool': nrm((DEPTH, DEC_BATCH, POOL_BUF, D_B), 1.0),
        'p_prompt': nrm((DEPTH, BATCH, SEQ, D_PLE), 1.0),
        'p_sample': nrm((DEPTH, DEC_BATCH, DEC_SEQ, D_PLE), 1.0),
        'norm_mix': gain((DEPTH, D_MODEL)),
        'w_in': nrm((DEPTH, D_MODEL, N_IN), D_MODEL ** -0.5),
        'tm_mix': unif((DEPTH, N_SHIFT), 0.0, 1.0),
        'w0': unif((DEPTH, D_A), -6.5, -1.5),
        'w2': nrm((DEPTH, LORA_W, D_A), 0.1 * LORA_W ** -0.5),
        'a0': nrm((DEPTH, D_A), 0.1),
        'a2': nrm((DEPTH, LORA_A, D_A), 0.5 * LORA_A ** -0.5),
        'g2': nrm((DEPTH, LORA_G, D_A), LORA_G ** -0.5),
        'k_k': 0.85 + nrm((DEPTH, D_A), 0.05),
        'k_a': gain((DEPTH, D_A)),
        'r_k': nrm((DEPTH, N_HEADS_A, HEAD_SIZE), 0.1),
        'ln_x_w': gain((DEPTH, D_A)),
        'ln_x_b': nrm((DEPTH, D_A), 0.02),
        'w_pool': nrm((DEPTH, N_POOL_GROUPS, POOL_GROUP, POOL_GROUP), POOL_GROUP ** -0.5),
        'pool_scale': gain((DEPTH, D_B)),
        'w_out_a': nrm((DEPTH, D_A, D_MODEL), D_A ** -0.5),
        'w_out_b': nrm((DEPTH, D_B, D_MODEL), D_B ** -0.5),
        'w_out': nrm((DEPTH, D_MODEL, D_MODEL), D_MODEL ** -0.5),
        'norm_ffn': gain((DEPTH, D_MODEL)),
        'w_router': nrm((DEPTH, D_MODEL, N_EXPERTS), D_MODEL ** -0.5),
        'b_router': nrm((DEPTH, N_EXPERTS), 0.01),
        'w_exp_in': nrm((DEPTH, N_EXPERTS, D_MODEL, 2 * D_FF), D_MODEL ** -0.5),
        'b_exp_in': nrm((DEPTH, N_EXPERTS, 2 * D_FF), 0.02),
        'w_exp_out': nrm((DEPTH, N_EXPERTS, D_FF, D_MODEL), D_FF ** -0.5),
        'b_exp_out': nrm((DEPTH, N_EXPERTS, D_MODEL), 0.02),
        'norm_ple': gain((DEPTH, D_MODEL)),
        'w_ple_gate': nrm((DEPTH, D_MODEL, D_MODEL), D_MODEL ** -0.5),
        'w_ple_proj': nrm((DEPTH, D_PLE, D_MODEL), D_PLE ** -0.5),
        'norm_final': gain((D_MODEL,)),
    }


def reference(x_prompt, x_sample, state_wkv, state_shift, state_pool, p_prompt, p_sample,
              norm_mix, w_in, tm_mix, w0, w2, a0, a2, g2, k_k, k_a, r_k, ln_x_w, ln_x_b,
              w_pool, pool_scale, w_out_a, w_out_b, w_out, norm_ffn,
              w_router, b_router, w_exp_in, b_exp_in, w_exp_out, b_exp_out,
              norm_ple, w_ple_gate, w_ple_proj, norm_final):
    f32 = jnp.float32
    n_prompt = x_prompt.shape[0]
    xp, xs = x_prompt, x_sample
    wkv_p, sh_p, pl_p, wkv_s, sh_s, pl_s = [], [], [], [], [], []
    for i in range(DEPTH):
        lw = tuple(t[i] for t in (norm_mix, w_in, tm_mix, w0, w2, a0, a2, g2, k_k, k_a, r_k,
                                  ln_x_w, ln_x_b, w_pool, pool_scale, w_out_a, w_out_b, w_out,
                                  norm_ffn, w_router, b_router, w_exp_in, b_exp_in, w_exp_out,
                                  b_exp_out, norm_ple, w_ple_gate, w_ple_proj))
        xp, s_new, h_last, pool_new = decoder_layer(
            xp, p_prompt[i],
            jnp.zeros((n_prompt, N_HEADS_A, HEAD_SIZE, HEAD_SIZE), f32),
            jnp.zeros((n_prompt, D_MODEL), f32),
            jnp.zeros((n_prompt, POOL_BUF, D_B), f32),
            0, *lw)
        wkv_p.append(s_new)
        sh_p.append(h_last)
        pl_p.append(pool_new)
        xs, s_new, h_last, pool_new = decoder_layer(
            xs, p_sample[i], state_wkv[i], state_shift[i], state_pool[i], PAST_LEN, *lw)
        wkv_s.append(s_new)
        sh_s.append(h_last)
        pl_s.append(pool_new)
    return (rms_norm(xp, norm_final), rms_norm(xs, norm_final),
            jnp.stack(wkv_p), jnp.stack(sh_p), jnp.stack(pl_p),
            jnp.stack(wkv_s), jnp.stack(sh_s), jnp.stack(pl_s))
```

```python
import functools
import math

import jax
import jax.numpy as jnp
from jax import lax
from jax.experimental import pallas as pl
from jax.experimental.pallas import tpu as pltpu

F32 = jnp.float32
BF16 = jnp.bfloat16

HEAD_SIZE = 64
GN_EPS = 64e-5
RMS_EPS = 1e-6
POOL_WINDOWS = (2, 4, 8, 16)
POOL_BUF = max(POOL_WINDOWS) - 1
TOP_K = 4
SWIGLU_ALPHA = 1.702
SWIGLU_LIMIT = 7.0
PAST_LEN = 16384

V7X_VMEM_BYTES = 64 * 1024 * 1024
VMEM_LIMIT = V7X_VMEM_BYTES - 8 * 1024 * 1024
LANES = 128
SEQ_PER_GROUP = 4
MOE_BM = 256


def _cparams(sem):
    return pltpu.CompilerParams(dimension_semantics=sem, vmem_limit_bytes=VMEM_LIMIT)


def _pick(n, pref):
    if n <= pref:
        return n
    b = pref - pref % 8
    while b >= 8:
        if n % b == 0:
            return b
        b -= 8
    return n


def _rms(x, g):
    return x * lax.rsqrt(jnp.mean(x * x, axis=-1, keepdims=True) + RMS_EPS) * g


def _dot(a, b):
    return jnp.dot(a, b, preferred_element_type=F32)


def _dot_hi(a, b):
    return jnp.dot(a, b, preferred_element_type=F32, precision=lax.Precision.HIGHEST)


def _proj_kernel(x_ref, g_ref, w_ref, o_ref, h_scr, *, normalize):
    @pl.when(pl.program_id(1) == 0)
    def _():
        x = x_ref[...]
        if normalize:
            x = _rms(x, g_ref[...])
        h_scr[...] = x.astype(BF16)

    o_ref[...] = _dot(h_scr[...], w_ref[...])


def _proj(x, g, w_bf, *, normalize):
    t, d = x.shape
    n = w_bf.shape[1]
    bm = _pick(t, 1024)
    bn = _pick(n, 512) if n % 128 == 0 else n
    return pl.pallas_call(
        functools.partial(_proj_kernel, normalize=normalize),
        out_shape=jax.ShapeDtypeStruct((t, n), F32),
        grid=(t // bm, n // bn),
        in_specs=[pl.BlockSpec((bm, d), lambda i, j: (i, 0)),
                  pl.BlockSpec((1, d), lambda i, j: (0, 0)),
                  pl.BlockSpec((d, bn), lambda i, j: (0, j))],
        out_specs=pl.BlockSpec((bm, bn), lambda i, j: (i, j)),
        scratch_shapes=[pltpu.VMEM((bm, d), BF16)],
        compiler_params=_cparams(("parallel", "arbitrary")),
        name="norm_proj" if normalize else "proj",
    )(x, g, w_bf)


def _rmsnorm_kernel(x_ref, g_ref, o_ref):
    o_ref[...] = _rms(x_ref[...], g_ref[...])


def _rmsnorm_rows(x, g):
    return pl.pallas_call(
        _rmsnorm_kernel,
        out_shape=jax.ShapeDtypeStruct(x.shape, F32),
        name="rmsnorm_rows",
    )(x, g)


def _prep_kernel(cur_ref, prev_ref, tm_ref, w0_ref, a0_ref, w2_ref, a2_ref, g2_ref,
                 r_ref, k_ref, v_ref, a_ref, w_ref, g_ref, *, d_a, lw, la):
    cur = cur_ref[...]
    m = cur + (prev_ref[...] - cur) * tm_ref[...]
    o1, o2, o3 = d_a, 2 * d_a, 3 * d_a
    o4, o5 = o3 + lw, o3 + lw + la
    r_ref[...] = m[:, :o1]
    k_ref[...] = m[:, o1:o2]
    v_ref[...] = m[:, o2:o3]
    xw, xa, xg = m[:, o3:o4], m[:, o4:o5], m[:, o5:]
    z = -(w0_ref[...] + _dot_hi(jnp.tanh(xw), w2_ref[...]))
    softplus = jnp.maximum(z, 0.0) + jnp.log1p(jnp.exp(-jnp.abs(z)))
    w_log = -softplus - 0.5
    w_ref[...] = jnp.exp(-jnp.exp(w_log))
    a_ref[...] = jax.nn.sigmoid(a0_ref[...] + _dot_hi(xa, a2_ref[...]))
    g_ref[...] = _dot_hi(jax.nn.sigmoid(xg), g2_ref[...])


def _prep(cur, prev, tm, w0, a0, w2p, a2p, g2, *, d_a, lw, la):
    t, ns = cur.shape
    bt = _pick(t, 256)
    row = lambda i: (i, 0)
    const = lambda i: (0, 0)
    out = jax.ShapeDtypeStruct((t, d_a), F32)
    return pl.pallas_call(
        functools.partial(_prep_kernel, d_a=d_a, lw=lw, la=la),
        out_shape=[out] * 6,
        grid=(t // bt,),
        in_specs=[pl.BlockSpec((bt, ns), row), pl.BlockSpec((bt, ns), row),
                  pl.BlockSpec((1, ns), const), pl.BlockSpec((1, d_a), const),
                  pl.BlockSpec((1, d_a), const), pl.BlockSpec(w2p.shape, const),
                  pl.BlockSpec(a2p.shape, const), pl.BlockSpec(g2.shape, const)],
        out_specs=[pl.BlockSpec((bt, d_a), row)] * 6,
        compiler_params=_cparams(("parallel",)),
        name="rwkv_prep",
    )(cur, prev, tm, w0, a0, w2p, a2p, g2)


def _scan_kernel(r_ref, k_ref, v_ref, a_ref, w_ref, s0_ref, kkc_ref, kac_ref, rkc_ref, lnw_ref, lnb_ref,
                 y_ref, st_ref, s_scr, r_scr, w_scr, nk_scr, b_scr, k2_scr, *, tb, n_key, n_val):
    @pl.when(pl.program_id(1) == 0)
    def _():
        s_scr[...] = s0_ref[0]

    def dup(x):
        return jnp.concatenate([x, x], axis=1)

    def lane_half_sum(x):
        xb = jnp.broadcast_to(x, (8, LANES))
        return (xb + pltpu.roll(xb, LANES // 2, axis=1))[0:1]

    def step(t, carry):
        r = dup(r_ref[0, t])
        kr = dup(k_ref[0, t])
        a = dup(a_ref[0, t])
        w = dup(w_ref[0, t])
        v = v_ref[0, t]
        kk = kr * kkc_ref[...]
        nrm = jnp.sqrt(jnp.sum(kk * kk, axis=0, keepdims=True))
        kkn = kk / jnp.maximum(nrm, 1e-12)
        k2 = kr * (1.0 + (a - 1.0) * kac_ref[...])
        r_scr[...] = r
        w_scr[...] = w
        nk_scr[...] = -kkn
        b_scr[...] = kkn * a
        k2_scr[...] = k2

        sa = jnp.zeros((n_val, LANES), F32)
        for kx in range(n_key):
            sa = sa + s_scr[kx] * nk_scr[pl.ds(kx, 1), :]
        y = jnp.zeros((n_val, LANES), F32)
        for kx in range(n_key):
            s_new = (s_scr[kx] * w_scr[pl.ds(kx, 1), :] + sa * b_scr[pl.ds(kx, 1), :]
                     + v * k2_scr[pl.ds(kx, 1), :])
            s_scr[kx] = s_new
            y = y + s_new * r_scr[pl.ds(kx, 1), :]

        bonus = jnp.sum(r * k2 * rkc_ref[...], axis=0, keepdims=True) * v
        inv_n = 1.0 / (2 * n_val)
        mu = lane_half_sum(jnp.sum(y, axis=0, keepdims=True)) * inv_n
        d = y - mu
        var = lane_half_sum(jnp.sum(d * d, axis=0, keepdims=True)) * inv_n
        y_ref[0, t] = d * lax.rsqrt(var + GN_EPS) * lnw_ref[...] + lnb_ref[...] + bonus
        return carry

    lax.fori_loop(0, tb, step, 0)

    @pl.when(pl.program_id(1) == pl.num_programs(1) - 1)
    def _():
        st_ref[0] = s_scr[...]


def _scan(r, k, v, a, w, s0, kkc, kac, rkc, lnw, lnb):
    g, l, n_key, n_chain = r.shape
    n_val = v.shape[2]
    tb = _pick(l, 32)
    seq = lambda gi, ti: (gi, ti, 0, 0)
    grp = lambda gi, ti: (gi, 0, 0, 0)
    const = lambda gi, ti: (0, 0)
    kv_blk = pl.BlockSpec((1, tb, n_key, n_chain), seq)
    val_blk = pl.BlockSpec((1, tb, n_val, LANES), seq)
    st_blk = pl.BlockSpec((1, n_key, n_val, LANES), grp)
    ckey = pl.BlockSpec((n_key, LANES), const)
    cval = pl.BlockSpec((n_val, LANES), const)
    tile = pltpu.VMEM((n_key, LANES), F32)
    return pl.pallas_call(
        functools.partial(_scan_kernel, tb=tb, n_key=n_key, n_val=n_val),
        out_shape=[jax.ShapeDtypeStruct(v.shape, F32), jax.ShapeDtypeStruct(s0.shape, F32)],
        grid=(g, l // tb),
        in_specs=[kv_blk, kv_blk, val_blk, kv_blk, kv_blk, st_blk, ckey, ckey, ckey, cval, cval],
        out_specs=[val_blk, st_blk],
        scratch_shapes=[pltpu.VMEM((n_key, n_val, LANES), F32), tile, tile, tile, tile, tile],
        compiler_params=_cparams(("parallel", "arbitrary")),
        name="wkv_scan",
    )(r, k, v, a, w, s0, kkc, kac, rkc, lnw, lnb)


def _to_scan_key(x, n_heads):
    b, l, _ = x.shape
    g = b // SEQ_PER_GROUP
    x = x.reshape(g, SEQ_PER_GROUP, l, n_heads, HEAD_SIZE).transpose(0, 2, 4, 1, 3)
    return x.reshape(g, l, HEAD_SIZE, SEQ_PER_GROUP * n_heads)


def _to_scan_val(x, n_heads):
    b, l, _ = x.shape
    g = b // SEQ_PER_GROUP
    x = x.reshape(g, SEQ_PER_GROUP, l, n_heads, 2, HEAD_SIZE // 2).transpose(0, 2, 5, 4, 1, 3)
    return x.reshape(g, l, HEAD_SIZE // 2, 2 * SEQ_PER_GROUP * n_heads)


def _from_scan_val(y, n_heads):
    g, l = y.shape[:2]
    y = y.reshape(g, l, HEAD_SIZE // 2, 2, SEQ_PER_GROUP, n_heads).transpose(0, 4, 1, 5, 3, 2)
    return y.reshape(g * SEQ_PER_GROUP, l, n_heads * HEAD_SIZE)


def _state_to_scan(s, n_heads):
    b = s.shape[0]
    g = b // SEQ_PER_GROUP
    s = s.reshape(g, SEQ_PER_GROUP, n_heads, 2, HEAD_SIZE // 2, HEAD_SIZE).transpose(0, 5, 4, 3, 1, 2)
    return s.reshape(g, HEAD_SIZE, HEAD_SIZE // 2, 2 * SEQ_PER_GROUP * n_heads)


def _state_from_scan(s, n_heads):
    g = s.shape[0]
    s = s.reshape(g, HEAD_SIZE, HEAD_SIZE // 2, 2, SEQ_PER_GROUP, n_heads).transpose(0, 4, 5, 3, 2, 1)
    return s.reshape(g * SEQ_PER_GROUP, n_heads, HEAD_SIZE, HEAD_SIZE)


def _key_const(p, n_heads):
    c = p.reshape(n_heads, HEAD_SIZE).T
    return jnp.tile(c, (1, 2 * SEQ_PER_GROUP))


def _val_const(p, n_heads):
    c = p.reshape(n_heads, 2, HEAD_SIZE // 2).transpose(2, 1, 0)
    c = jnp.broadcast_to(c[:, :, None, :], (HEAD_SIZE // 2, 2, SEQ_PER_GROUP, n_heads))
    return c.reshape(HEAD_SIZE // 2, 2 * SEQ_PER_GROUP * n_heads)


def _pool_kernel(u_ref, up_ref, wp_ref, sc_ref, o_ref, ext_scr, *, lb, pos0, grp):
    ti = pl.program_id(1)
    nb = u_ref.shape[0]
    hist = POOL_BUF + 1

    @pl.when(ti == 0)
    def _():
        ext_scr[:, 0:1, :] = jnp.zeros((nb, 1, u_ref.shape[2]), F32)
        ext_scr[:, 1:hist, :] = up_ref[...]

    @pl.when(ti > 0)
    def _():
        ext_scr[:, 0:hist, :] = ext_scr[:, lb:lb + hist, :]

    u = u_ref[...]
    ext_scr[:, hist:hist + lb, :] = u
    pos = pos0 + ti * lb + lax.broadcasted_iota(jnp.int32, (1, lb, 1), 1)
    for j, win in enumerate(POOL_WINDOWS):
        c0, c1 = j * grp, (j + 1) * grp
        acc = u[:, :, c0:c1]
        for dlt in range(1, win):
            acc = acc + ext_scr[:, hist - dlt:hist - dlt + lb, c0:c1]
        cnt = jnp.minimum(pos + 1, win).astype(F32)
        pooled = acc / cnt - u[:, :, c0:c1]
        mixed = _dot(pooled.reshape(nb * lb, grp).astype(BF16), wp_ref[j])
        o_ref[:, :, c0:c1] = mixed.reshape(nb, lb, grp) * sc_ref[:, c0:c1]


def _pool(u, u_prev, w_pool_bf, pool_scale, pos0):
    b, l, d_b = u.shape
    grp = d_b // len(POOL_WINDOWS)
    lb = _pick(l, 512)
    nb = _pick(b, max(1, 128 // lb)) if lb < 128 else 1
    blk = pl.BlockSpec((nb, lb, d_b), lambda bi, ti: (bi, ti, 0))
    return pl.pallas_call(
        functools.partial(_pool_kernel, lb=lb, pos0=pos0, grp=grp),
        out_shape=jax.ShapeDtypeStruct(u.shape, F32),
        grid=(b // nb, l // lb),
        in_specs=[blk,
                  pl.BlockSpec((nb, POOL_BUF, d_b), lambda bi, ti: (bi, 0, 0)),
                  pl.BlockSpec(w_pool_bf.shape, lambda bi, ti: (0, 0, 0)),
                  pl.BlockSpec((1, d_b), lambda bi, ti: (0, 0))],
        out_specs=blk,
        scratch_shapes=[pltpu.VMEM((nb, lb + POOL_BUF + 1, d_b), F32)],
        compiler_params=_cparams(("parallel", "arbitrary")),
        name="pool_mix",
    )(u, u_prev, w_pool_bf, pool_scale)


def _merge_kernel(yo_ref, g_ref, ob_ref, ga_ref, gb_ref, wa_ref, wb_ref, o_ref):
    o_a = (yo_ref[...] * g_ref[...]).astype(BF16)
    pa = _dot(o_a, wa_ref[...])
    pb = _dot(ob_ref[...].astype(BF16), wb_ref[...])
    o_ref[...] = (jax.nn.sigmoid(ga_ref[...]) * pa + jax.nn.sigmoid(gb_ref[...]) * pb).astype(BF16)


def _merge(yo, g, o_b, gates, wa_bf, wb_bf):
    t, d_a = yo.shape
    d = wa_bf.shape[1]
    bt = _pick(t, 512)
    row = lambda i: (i, 0)
    const = lambda i: (0, 0)
    return pl.pallas_call(
        _merge_kernel,
        out_shape=jax.ShapeDtypeStruct((t, d), BF16),
        grid=(t // bt,),
        in_specs=[pl.BlockSpec((bt, d_a), row), pl.BlockSpec((bt, d_a), row),
                  pl.BlockSpec((bt, o_b.shape[1]), row),
                  pl.BlockSpec((bt, d), lambda i: (i, 0)), pl.BlockSpec((bt, d), lambda i: (i, 1)),
                  pl.BlockSpec(wa_bf.shape, const), pl.BlockSpec(wb_bf.shape, const)],
        out_specs=pl.BlockSpec((bt, d), row),
        compiler_params=_cparams(("parallel",)),
        name="branch_merge",
    )(yo, g, o_b, gates, gates, wa_bf, wb_bf)


def _outproj_router_kernel(x_ref, m_ref, wo_ref, nf_ref, wr_ref, br_ref,
                           x1_ref, h2_ref, idx_ref, gate_ref, *, n_exp):
    x1 = x_ref[...] + _dot(m_ref[...], wo_ref[...])
    x1_ref[...] = x1
    h2 = _rms(x1, nf_ref[...])
    h2_ref[...] = h2
    logits = _dot_hi(h2, wr_ref[...]) + br_ref[...]
    bt = logits.shape[0]
    lane = lax.broadcasted_iota(jnp.int32, (bt, n_exp), 1)
    out_lane = lax.broadcasted_iota(jnp.int32, (bt, LANES), 1)
    idx_out = jnp.zeros((bt, LANES), jnp.int32)
    val_out = jnp.zeros((bt, LANES), F32)
    vals = []
    for kx in range(TOP_K):
        mx = jnp.max(logits, axis=-1, keepdims=True)
        sel = jnp.min(jnp.where(logits == mx, lane, n_exp), axis=-1, keepdims=True)
        idx_out = jnp.where(out_lane == kx, sel, idx_out)
        vals.append(mx)
        logits = jnp.where(lane == sel, -jnp.inf, logits)
    exps = [jnp.exp(vx - vals[0]) for vx in vals]
    den = exps[0] + exps[1] + exps[2] + exps[3]
    for kx in range(TOP_K):
        val_out = jnp.where(out_lane == kx, exps[kx] / den, val_out)
    idx_ref[...] = idx_out
    gate_ref[...] = val_out


def _outproj_router(x, merged, wo_bf, norm_ffn, w_router, b_router):
    t, d = x.shape
    n_exp = w_router.shape[1]
    bt = _pick(t, 512)
    row = lambda i: (i, 0)
    const = lambda i: (0, 0)
    return pl.pallas_call(
        functools.partial(_outproj_router_kernel, n_exp=n_exp),
        out_shape=[jax.ShapeDtypeStruct((t, d), F32), jax.ShapeDtypeStruct((t, d), F32),
                   jax.ShapeDtypeStruct((t, LANES), jnp.int32), jax.ShapeDtypeStruct((t, LANES), F32)],
        grid=(t // bt,),
        in_specs=[pl.BlockSpec((bt, d), row), pl.BlockSpec((bt, d), row),
                  pl.BlockSpec((d, d), const), pl.BlockSpec((1, d), const),
                  pl.BlockSpec((d, n_exp), const), pl.BlockSpec((1, n_exp), const)],
        out_specs=[pl.BlockSpec((bt, d), row), pl.BlockSpec((bt, d), row),
                   pl.BlockSpec((bt, LANES), row), pl.BlockSpec((bt, LANES), row)],
        compiler_params=_cparams(("parallel",)),
        name="outproj_router",
    )(x, merged, wo_bf, norm_ffn, w_router, b_router)


def _row_copy(src_hbm, dst_vmem, sem, src_row, dst_row):
    return pltpu.make_async_copy(src_hbm.at[pl.ds(src_row, 1)], dst_vmem.at[pl.ds(dst_row, 1)], sem)


def _dispatch_kernel(rowtok_ref, x_hbm, o_ref, buf, sem, *, bm):
    base = pl.program_id(0) * bm

    def issue(r, c):
        _row_copy(x_hbm, buf, sem, rowtok_ref[base + r], r).start()
        return c

    lax.fori_loop(0, bm, issue, 0)

    def drain(r, c):
        _row_copy(x_hbm, buf, sem, 0, r).wait()
        return c

    lax.fori_loop(0, bm, drain, 0)
    o_ref[...] = buf[...].astype(BF16)


def _dispatch(row_tok, h2, n_rows, bm):
    d = h2.shape[1]
    return pl.pallas_call(
        functools.partial(_dispatch_kernel, bm=bm),
        out_shape=jax.ShapeDtypeStruct((n_rows, d), BF16),
        grid_spec=pltpu.PrefetchScalarGridSpec(
            num_scalar_prefetch=1,
            grid=(n_rows // bm,),
            in_specs=[pl.BlockSpec(memory_space=pl.ANY)],
            out_specs=pl.BlockSpec((bm, d), lambda i, rt: (i, 0)),
            scratch_shapes=[pltpu.VMEM((bm, d), F32), pltpu.SemaphoreType.DMA(())]),
        compiler_params=_cparams(("arbitrary",)),
        name="moe_dispatch",
    )(row_tok, h2)


def _expert_in_kernel(be_ref, bv_ref, x_ref, wg_ref, wu_ref, bg_ref, bu_ref, o_ref, wg_scr, wu_scr):
    i = pl.program_id(1)
    prev = be_ref[jnp.maximum(i - 1, 0)]
    fresh = jnp.logical_or(i == 0, be_ref[i] != prev)

    @pl.when(fresh)
    def _():
        wg_scr[...] = wg_ref[0].astype(BF16)
        wu_scr[...] = wu_ref[0].astype(BF16)

    @pl.when(bv_ref[i] > 0)
    def _():
        x = x_ref[...]
        gate = jnp.minimum(_dot(x, wg_scr[...]) + bg_ref[0], SWIGLU_LIMIT)
        up = jnp.clip(_dot(x, wu_scr[...]) + bu_ref[0], -SWIGLU_LIMIT, SWIGLU_LIMIT)
        glu = gate * jax.nn.sigmoid(gate * SWIGLU_ALPHA)
        o_ref[...] = ((up + 1.0) * glu).astype(BF16)

    @pl.when(bv_ref[i] == 0)
    def _():
        o_ref[...] = jnp.zeros(o_ref.shape, BF16)


def _expert_in(block_expert, block_valid, rows, w_exp_in, b_exp_in, bm):
    n_rows, d = rows.shape
    n_exp, _, two_ff = w_exp_in.shape
    d_ff = two_ff // 2
    tn = _pick(d_ff, 1024)
    nj = d_ff // tn
    b3 = b_exp_in.reshape(n_exp, 1, two_ff)
    return pl.pallas_call(
        _expert_in_kernel,
        out_shape=jax.ShapeDtypeStruct((n_rows, d_ff), BF16),
        grid_spec=pltpu.PrefetchScalarGridSpec(
            num_scalar_prefetch=2,
            grid=(nj, n_rows // bm),
            in_specs=[pl.BlockSpec((bm, d), lambda j, i, be, bv: (i, 0)),
                      pl.BlockSpec((1, d, tn), lambda j, i, be, bv: (be[i], 0, j)),
                      pl.BlockSpec((1, d, tn), lambda j, i, be, bv: (be[i], 0, nj + j)),
                      pl.BlockSpec((1, 1, tn), lambda j, i, be, bv: (be[i], 0, j)),
                      pl.BlockSpec((1, 1, tn), lambda j, i, be, bv: (be[i], 0, nj + j))],
            out_specs=pl.BlockSpec((bm, tn), lambda j, i, be, bv: (i, j)),
            scratch_shapes=[pltpu.VMEM((d, tn), BF16), pltpu.VMEM((d, tn), BF16)]),
        compiler_params=_cparams(("arbitrary", "arbitrary")),
        name="expert_in",
    )(block_expert, block_valid, rows, w_exp_in, w_exp_in, b3, b3)


def _expert_out_kernel(be_ref, bv_ref, h_ref, w_ref, b_ref, o_ref, w_scr):
    i = pl.program_id(0)
    prev = be_ref[jnp.maximum(i - 1, 0)]
    fresh = jnp.logical_or(i == 0, be_ref[i] != prev)

    @pl.when(fresh)
    def _():
        w_scr[...] = w_ref[0].astype(BF16)

    @pl.when(bv_ref[i] > 0)
    def _():
        o_ref[...] = _dot(h_ref[...], w_scr[...]) + b_ref[0]

    @pl.when(bv_ref[i] == 0)
    def _():
        o_ref[...] = jnp.zeros(o_ref.shape, F32)


def _expert_out(block_expert, block_valid, hid, w_exp_out, b_exp_out, bm):
    n_rows, d_ff = hid.shape
    n_exp, _, d = w_exp_out.shape
    b3 = b_exp_out.reshape(n_exp, 1, d)
    return pl.pallas_call(
        _expert_out_kernel,
        out_shape=jax.ShapeDtypeStruct((n_rows, d), F32),
        grid_spec=pltpu.PrefetchScalarGridSpec(
            num_scalar_prefetch=2,
            grid=(n_rows // bm,),
            in_specs=[pl.BlockSpec((bm, d_ff), lambda i, be, bv: (i, 0)),
                      pl.BlockSpec((1, d_ff, d), lambda i, be, bv: (be[i], 0, 0)),
                      pl.BlockSpec((1, 1, d), lambda i, be, bv: (be[i], 0, 0))],
            out_specs=pl.BlockSpec((bm, d), lambda i, be, bv: (i, 0)),
            scratch_shapes=[pltpu.VMEM((d_ff, d), BF16)]),
        compiler_params=_cparams(("arbitrary",)),
        name="expert_out",
    )(block_expert, block_valid, hid, w_exp_out, b3)


def _combine_ple_kernel(pos_ref, x1_ref, gate_ref, p_ref, np_ref, wg_ref, wp_ref, nfin_ref, eo_hbm,
                        o_ref, buf, sem, *, bt):
    base = pl.program_id(0) * bt

    def issue(r, c):
        for kx in range(TOP_K):
            _row_copy(eo_hbm, buf.at[kx], sem, pos_ref[(base + r) * TOP_K + kx], r).start()
        return c

    lax.fori_loop(0, bt, issue, 0)

    def drain(r, c):
        for kx in range(TOP_K):
            _row_copy(eo_hbm, buf.at[kx], sem, 0, r).wait()
        return c

    lax.fori_loop(0, bt, drain, 0)
    gates = gate_ref[...]
    x2 = x1_ref[...]
    for kx in range(TOP_K):
        x2 = x2 + gates[:, kx:kx + 1] * buf[kx]
    hp = _rms(x2, np_ref[...]).astype(BF16)
    x3 = x2 + jax.nn.sigmoid(_dot(hp, wg_ref[...])) * _dot(p_ref[...].astype(BF16), wp_ref[...])
    o_ref[...] = _rms(x3, nfin_ref[...])


def _combine_ple(pos, x1, gates, p, norm_ple, wg_bf, wp_bf, norm_final, eo):
    t, d = x1.shape
    d_ple = p.shape[1]
    bt = _pick(t, 256)
    row = lambda i, ps: (i, 0)
    const = lambda i, ps: (0, 0)
    return pl.pallas_call(
        functools.partial(_combine_ple_kernel, bt=bt),
        out_shape=jax.ShapeDtypeStruct((t, d), F32),
        grid_spec=pltpu.PrefetchScalarGridSpec(
            num_scalar_prefetch=1,
            grid=(t // bt,),
            in_specs=[pl.BlockSpec((bt, d), row), pl.BlockSpec((bt, LANES), row),
                      pl.BlockSpec((bt, d_ple), row), pl.BlockSpec((1, d), const),
                      pl.BlockSpec((d, d), const), pl.BlockSpec((d_ple, d), const),
                      pl.BlockSpec((1, d), const), pl.BlockSpec(memory_space=pl.ANY)],
            out_specs=pl.BlockSpec((bt, d), row),
            scratch_shapes=[pltpu.VMEM((TOP_K, bt, d), F32), pltpu.SemaphoreType.DMA(())]),
        compiler_params=_cparams(("arbitrary",)),
        name="combine_ple",
    )(pos, x1, gates, p, norm_ple, wg_bf, wp_bf, norm_final, eo)


def _routing(top_idx, n_exp, bm):
    t = top_idx.shape[0]
    n_assign = t * TOP_K
    e_flat = top_idx.reshape(-1)
    onehot = (e_flat[:, None] == jnp.arange(n_exp, dtype=jnp.int32)[None, :]).astype(jnp.int32)
    csum = jnp.cumsum(onehot, axis=0)
    rank = jnp.take_along_axis(csum, e_flat[:, None], axis=1)[:, 0] - 1
    counts = csum[-1]
    padded = (counts + bm - 1) // bm * bm
    pad_end = jnp.cumsum(padded)
    pad_start = pad_end - padded
    pos = (pad_start[e_flat] + rank).astype(jnp.int32)
    n_blocks = -(-n_assign // bm) + n_exp
    n_rows = n_blocks * bm
    row_tok = jnp.zeros((n_rows,), jnp.int32).at[pos].set(jnp.arange(n_assign, dtype=jnp.int32) // TOP_K)
    block_off = jnp.arange(n_blocks, dtype=jnp.int32) * bm
    block_expert = jnp.minimum(jnp.searchsorted(pad_end, block_off, side='right'), n_exp - 1).astype(jnp.int32)
    block_valid = (block_off < pad_end[-1]).astype(jnp.int32)
    return pos, row_tok, block_expert, block_valid, n_rows


def kernel(x_prompt, x_sample, state_wkv, state_shift, state_pool, p_prompt, p_sample, norm_mix, w_in, tm_mix, w0, w2, a0, a2, g2, k_k, k_a, r_k, ln_x_w, ln_x_b, w_pool, pool_scale, w_out_a, w_out_b, w_out, norm_ffn, w_router, b_router, w_exp_in, b_exp_in, w_exp_out, b_exp_out, norm_ple, w_ple_gate, w_ple_proj, norm_final):
    depth = w_in.shape[0]
    assert depth == 1, "single-layer step"
    bp, lp, d = x_prompt.shape
    bs, ls, _ = x_sample.shape
    d_a = w0.shape[1]
    d_b = pool_scale.shape[1]
    n_heads = d_a // HEAD_SIZE
    lw, la, lg = w2.shape[1], a2.shape[1], g2.shape[1]
    n_exp = w_router.shape[2]
    assert SEQ_PER_GROUP * n_heads * 2 == LANES and bp % SEQ_PER_GROUP == 0 and bs % SEQ_PER_GROUP == 0
    tp, ts = bp * lp, bs * ls
    t = tp + ts

    wi = w_in[0]
    o3 = 3 * d_a
    lwp, lap = -(-lw // LANES) * LANES, -(-la // LANES) * LANES
    zc = lambda n: jnp.zeros((d, n), F32)
    w_shift = jnp.concatenate([wi[:, :o3], wi[:, o3:o3 + lw], zc(lwp - lw), wi[:, o3 + lw:o3 + lw + la],
                               zc(lap - la), wi[:, o3 + lw + la:o3 + lw + la + lg]], axis=1).astype(BF16)
    n_shift = o3 + lw + la + lg
    w_u = wi[:, n_shift:n_shift + d_b].astype(BF16)
    w_g = wi[:, n_shift + d_b:].astype(BF16)
    tm = tm_mix[0]
    zt = lambda n: jnp.zeros((n,), F32)
    tm_p = jnp.concatenate([tm[:o3], tm[o3:o3 + lw], zt(lwp - lw), tm[o3 + lw:o3 + lw + la], zt(lap - la),
                            tm[o3 + lw + la:]])[None, :]
    w2p = jnp.concatenate([w2[0], jnp.zeros((lwp - lw, d_a), F32)], axis=0)
    a2p = jnp.concatenate([a2[0], jnp.zeros((lap - la, d_a), F32)], axis=0)

    x_all = jnp.concatenate([x_prompt.reshape(tp, d), x_sample.reshape(ts, d)], axis=0)
    p_all = jnp.concatenate([p_prompt[0].reshape(tp, -1), p_sample[0].reshape(ts, -1)], axis=0)
    nm = norm_mix[0][None, :]

    cur = _proj(x_all, nm, w_shift, normalize=True)
    u_all = _proj(x_all, nm, w_u, normalize=True)
    gates_ab = _proj(x_all, nm, w_g, normalize=True)
    first_prev_s = _proj(state_shift[0], nm, w_shift, normalize=False)
    h_last = _rmsnorm_rows(jnp.concatenate([x_prompt[:, -1], x_sample[:, -1]], axis=0), nm)

    nsp = cur.shape[1]
    cur_p = cur[:tp].reshape(bp, lp, nsp)
    cur_s = cur[tp:].reshape(bs, ls, nsp)
    prev_p = jnp.concatenate([jnp.zeros((bp, 1, nsp), F32), cur_p[:, :-1]], axis=1)
    prev_s = jnp.concatenate([first_prev_s[:, None], cur_s[:, :-1]], axis=1)
    prev = jnp.concatenate([prev_p.reshape(tp, nsp), prev_s.reshape(ts, nsp)], axis=0)

    r, k, v, a, w, g = _prep(cur, prev, tm_p, w0, a0, w2p, a2p, g2[0], d_a=d_a, lw=lwp, la=lap)

    kkc, kac, rkc = (_key_const(c, n_heads) for c in (k_k[0], k_a[0], r_k[0].reshape(-1)))
    lnw, lnb = _val_const(ln_x_w[0], n_heads), _val_const(ln_x_b[0], n_heads)

    def run_scan(lo, hi, b, l, s0):
        sl = lambda z: z[lo:hi].reshape(b, l, d_a)
        yo, st = _scan(_to_scan_key(sl(r), n_heads), _to_scan_key(sl(k), n_heads),
                       _to_scan_val(sl(v), n_heads), _to_scan_key(sl(a), n_heads),
                       _to_scan_key(sl(w), n_heads), _state_to_scan(s0, n_heads), kkc, kac, rkc, lnw, lnb)
        return _from_scan_val(yo, n_heads).reshape(b * l, d_a), _state_from_scan(st, n_heads)

    yo_p, wkv_p = run_scan(0, tp, bp, lp, jnp.zeros((bp, n_heads, HEAD_SIZE, HEAD_SIZE), F32))
    yo_s, wkv_s = run_scan(tp, t, bs, ls, state_wkv[0])
    yo = jnp.concatenate([yo_p, yo_s], axis=0)

    wp_bf = w_pool[0].astype(BF16)
    u_p = u_all[:tp].reshape(bp, lp, d_b)
    u_s = u_all[tp:].reshape(bs, ls, d_b)
    ob_p = _pool(u_p, jnp.zeros((bp, POOL_BUF, d_b), F32), wp_bf, pool_scale, 0)
    ob_s = _pool(u_s, state_pool[0], wp_bf, pool_scale, PAST_LEN)
    o_b = jnp.concatenate([ob_p.reshape(tp, d_b), ob_s.reshape(ts, d_b)], axis=0)
    pool_p = jnp.concatenate([jnp.zeros((bp, POOL_BUF, d_b), F32), u_p], axis=1)[:, -POOL_BUF:]
    pool_s = jnp.concatenate([state_pool[0], u_s], axis=1)[:, -POOL_BUF:]

    merged = _merge(yo, g, o_b, gates_ab, w_out_a[0].astype(BF16), w_out_b[0].astype(BF16))
    x1, h2, top_idx, gate_vals = _outproj_router(x_all, merged, w_out[0].astype(BF16), norm_ffn,
                                                 w_router[0], b_router)

    pos, row_tok, block_expert, block_valid, n_rows = _routing(top_idx[:, :TOP_K], n_exp, MOE_BM)
    rows = _dispatch(row_tok, h2, n_rows, MOE_BM)
    hid = _expert_in(block_expert, block_valid, rows, w_exp_in[0], b_exp_in[0], MOE_BM)
    eo = _expert_out(block_expert, block_valid, hid, w_exp_out[0], b_exp_out[0], MOE_BM)
    y = _combine_ple(pos, x1, gate_vals, p_all, norm_ple, w_ple_gate[0].astype(BF16),
                     w_ple_proj[0].astype(BF16), norm_final[None, :], eo)

    return (y[:tp].reshape(bp, lp, d), y[tp:].reshape(bs, ls, d),
            wkv_p[None], h_last[:bp][None], pool_p[None],
            wkv_s[None], h_last[bp:][None], pool_s[None])
```

```python
import functools
import math

import jax
import jax.numpy as jnp
from jax import lax
from jax.experimental import pallas as pl
from jax.experimental.pallas import tpu as pltpu

F32 = jnp.float32
BF16 = jnp.bfloat16

HEAD_SIZE = 64
GN_EPS = 64e-5
RMS_EPS = 1e-6
POOL_WINDOWS = (2, 4, 8, 16)
POOL_BUF = max(POOL_WINDOWS) - 1
TOP_K = 4
SWIGLU_ALPHA = 1.702
SWIGLU_LIMIT = 7.0
PAST_LEN = 16384

V7X_VMEM_BYTES = 64 * 1024 * 1024
VMEM_LIMIT = V7X_VMEM_BYTES - 8 * 1024 * 1024
LANES = 128
SUBLANES = 8
SEQ_PER_GROUP = 4
MOE_BM = 256


def _cparams(sem):
    return pltpu.CompilerParams(dimension_semantics=sem, vmem_limit_bytes=VMEM_LIMIT)


def _pick(n, pref):
    if n <= pref:
        return n
    b = pref - pref % 8
    while b >= 8:
        if n % b == 0:
            return b
        b -= 8
    return n


def _rms(x, g):
    return x * lax.rsqrt(jnp.mean(x * x, axis=-1, keepdims=True) + RMS_EPS) * g


def _dot(a, b):
    return jnp.dot(a, b, preferred_element_type=F32)


def _dot_hi(a, b):
    return jnp.dot(a, b, preferred_element_type=F32, precision=lax.Precision.HIGHEST)


def _proj_kernel(x_ref, g_ref, w_ref, o_ref, h_scr, *, normalize):
    @pl.when(pl.program_id(1) == 0)
    def _():
        x = x_ref[...]
        if normalize:
            x = _rms(x, g_ref[...])
        h_scr[...] = x.astype(BF16)

    o_ref[...] = _dot(h_scr[...], w_ref[...])


def _proj(x, g, w_bf, *, normalize):
    t, d = x.shape
    n = w_bf.shape[1]
    bm = _pick(t, 1024)
    bn = _pick(n, 512) if n % 128 == 0 else n
    return pl.pallas_call(
        functools.partial(_proj_kernel, normalize=normalize),
        out_shape=jax.ShapeDtypeStruct((t, n), F32),
        grid=(t // bm, n // bn),
        in_specs=[pl.BlockSpec((bm, d), lambda i, j: (i, 0)),
                  pl.BlockSpec((1, d), lambda i, j: (0, 0)),
                  pl.BlockSpec((d, bn), lambda i, j: (0, j))],
        out_specs=pl.BlockSpec((bm, bn), lambda i, j: (i, j)),
        scratch_shapes=[pltpu.VMEM((bm, d), BF16)],
        compiler_params=_cparams(("parallel", "arbitrary")),
        name="norm_proj" if normalize else "proj",
    )(x, g, w_bf)


def _rmsnorm_kernel(x_ref, g_ref, o_ref):
    o_ref[...] = _rms(x_ref[...], g_ref[...])


def _rmsnorm_rows(x, g):
    return pl.pallas_call(
        _rmsnorm_kernel,
        out_shape=jax.ShapeDtypeStruct(x.shape, F32),
        name="rmsnorm_rows",
    )(x, g)


def _prep_kernel(cur_ref, prev_ref, tm_ref, w0_ref, a0_ref, w2_ref, a2_ref, g2_ref,
                 r_ref, k_ref, v_ref, a_ref, w_ref, g_ref, *, d_a, lw, la):
    cur = cur_ref[...]
    m = cur + (prev_ref[...] - cur) * tm_ref[...]
    o1, o2, o3 = d_a, 2 * d_a, 3 * d_a
    o4, o5 = o3 + lw, o3 + lw + la
    r_ref[...] = m[:, :o1]
    k_ref[...] = m[:, o1:o2]
    v_ref[...] = m[:, o2:o3]
    xw, xa, xg = m[:, o3:o4], m[:, o4:o5], m[:, o5:]
    z = -(w0_ref[...] + _dot_hi(jnp.tanh(xw), w2_ref[...]))
    softplus = jnp.maximum(z, 0.0) + jnp.log1p(jnp.exp(-jnp.abs(z)))
    w_log = -softplus - 0.5
    w_ref[...] = jnp.exp(-jnp.exp(w_log))
    a_ref[...] = jax.nn.sigmoid(a0_ref[...] + _dot_hi(xa, a2_ref[...]))
    g_ref[...] = _dot_hi(jax.nn.sigmoid(xg), g2_ref[...])


def _prep(cur, prev, tm, w0, a0, w2p, a2p, g2, *, d_a, lw, la):
    t, ns = cur.shape
    bt = _pick(t, 256)
    row = lambda i: (i, 0)
    const = lambda i: (0, 0)
    out = jax.ShapeDtypeStruct((t, d_a), F32)
    return pl.pallas_call(
        functools.partial(_prep_kernel, d_a=d_a, lw=lw, la=la),
        out_shape=[out] * 6,
        grid=(t // bt,),
        in_specs=[pl.BlockSpec((bt, ns), row), pl.BlockSpec((bt, ns), row),
                  pl.BlockSpec((1, ns), const), pl.BlockSpec((1, d_a), const),
                  pl.BlockSpec((1, d_a), const), pl.BlockSpec(w2p.shape, const),
                  pl.BlockSpec(a2p.shape, const), pl.BlockSpec(g2.shape, const)],
        out_specs=[pl.BlockSpec((bt, d_a), row)] * 6,
        compiler_params=_cparams(("parallel",)),
        name="rwkv_prep",
    )(cur, prev, tm, w0, a0, w2p, a2p, g2)


def _scan_kernel(r_ref, k_ref, v_ref, a_ref, w_ref, s0_ref, kkc_ref, kac_ref, rkc_ref, lnw_ref, lnb_ref,
                 y_ref, st_ref, s_scr, r_scr, w_scr, nk_scr, b_scr, k2_scr, bon_scr, yraw_scr,
                 *, tb, n_key, n_val):
    half = LANES // 2

    @pl.when(pl.program_id(1) == 0)
    def _():
        s_scr[...] = s0_ref[0]

    def unpack(x):
        lo, hi = x[:, :half], x[:, half:]
        return jnp.concatenate([jnp.concatenate([lo, lo], axis=1), jnp.concatenate([hi, hi], axis=1)], axis=0)

    def lane_half_sum(x):
        xb = jnp.broadcast_to(x, (SUBLANES, LANES))
        return (xb + pltpu.roll(xb, half, axis=1))[0:1]

    def prep(t, carry):
        r = unpack(r_ref[0, t])
        kr = unpack(k_ref[0, t])
        a = unpack(a_ref[0, t])
        kk = kr * kkc_ref[...]
        nrm = jnp.sqrt(jnp.sum(kk * kk, axis=0, keepdims=True))
        kkn = kk / jnp.maximum(nrm, 1e-12)
        k2 = kr * (1.0 + (a - 1.0) * kac_ref[...])
        r_scr[t] = r
        w_scr[t] = unpack(w_ref[0, t])
        nk_scr[t] = -kkn
        b_scr[t] = kkn * a
        k2_scr[t] = k2
        bon_scr[t] = jnp.sum(r * k2 * rkc_ref[...], axis=0, keepdims=True)
        return carry

    lax.fori_loop(0, tb, prep, 0, unroll=4)
    nk_scr[tb] = jnp.zeros((n_key, LANES), F32)

    zero = jnp.zeros((n_val, LANES), F32)
    acc = [zero, zero]
    for kx in range(n_key):
        acc[kx % 2] = acc[kx % 2] + s_scr[kx] * nk_scr[0, pl.ds(kx, 1), :]

    def step(t, sa):
        v = v_ref[0, t]
        y = [zero, zero]
        sn = [zero, zero]
        for kx in range(n_key):
            row = pl.ds(kx, 1)
            s_new = s_scr[kx] * w_scr[t, row, :] + sa * b_scr[t, row, :] + v * k2_scr[t, row, :]
            s_scr[kx] = s_new
            y[kx % 2] = y[kx % 2] + s_new * r_scr[t, row, :]
            sn[kx % 2] = sn[kx % 2] + s_new * nk_scr[t + 1, row, :]
        yraw_scr[t] = y[0] + y[1]
        return sn[0] + sn[1]

    lax.fori_loop(0, tb, step, acc[0] + acc[1])

    def post(t, carry):
        y = yraw_scr[t]
        inv_n = 1.0 / (2 * n_val)
        mu = lane_half_sum(jnp.sum(y, axis=0, keepdims=True)) * inv_n
        d = y - mu
        var = lane_half_sum(jnp.sum(d * d, axis=0, keepdims=True)) * inv_n
        y_ref[0, t] = d * lax.rsqrt(var + GN_EPS) * lnw_ref[...] + lnb_ref[...] + bon_scr[t] * v_ref[0, t]
        return carry

    lax.fori_loop(0, tb, post, 0, unroll=4)

    @pl.when(pl.program_id(1) == pl.num_programs(1) - 1)
    def _():
        st_ref[0] = s_scr[...]


def _scan(r, k, v, a, w, s0, kkc, kac, rkc, lnw, lnb):
    g, l, n_val, _ = v.shape
    n_key = 2 * r.shape[2]
    tb = _pick(l, 32)
    seq = lambda gi, ti: (gi, ti, 0, 0)
    grp = lambda gi, ti: (gi, 0, 0, 0)
    const = lambda gi, ti: (0, 0)
    key_blk = pl.BlockSpec((1, tb, n_key // 2, LANES), seq)
    val_blk = pl.BlockSpec((1, tb, n_val, LANES), seq)
    st_blk = pl.BlockSpec((1, n_key, n_val, LANES), grp)
    ckey = pl.BlockSpec((n_key, LANES), const)
    cval = pl.BlockSpec((n_val, LANES), const)
    tiles = pltpu.VMEM((tb, n_key, LANES), F32)
    return pl.pallas_call(
        functools.partial(_scan_kernel, tb=tb, n_key=n_key, n_val=n_val),
        out_shape=[jax.ShapeDtypeStruct(v.shape, F32), jax.ShapeDtypeStruct(s0.shape, F32)],
        grid=(g, l // tb),
        in_specs=[key_blk, key_blk, val_blk, key_blk, key_blk, st_blk, ckey, ckey, ckey, cval, cval],
        out_specs=[val_blk, st_blk],
        scratch_shapes=[pltpu.VMEM((n_key, n_val, LANES), F32), tiles, tiles,
                        pltpu.VMEM((tb + 1, n_key, LANES), F32), tiles, tiles,
                        pltpu.VMEM((tb, 1, LANES), F32), pltpu.VMEM((tb, n_val, LANES), F32)],
        compiler_params=_cparams(("parallel", "arbitrary")),
        name="wkv_scan",
    )(r, k, v, a, w, s0, kkc, kac, rkc, lnw, lnb)


def _to_scan_val(x, n_heads):
    b, l, _ = x.shape
    g = b // SEQ_PER_GROUP
    x = x.reshape(g, SEQ_PER_GROUP, l, n_heads, 2, HEAD_SIZE // 2).transpose(0, 2, 5, 4, 1, 3)
    return x.reshape(g, l, HEAD_SIZE // 2, 2 * SEQ_PER_GROUP * n_heads)


def _from_scan_val(y, n_heads):
    g, l = y.shape[:2]
    y = y.reshape(g, l, HEAD_SIZE // 2, 2, SEQ_PER_GROUP, n_heads).transpose(0, 4, 1, 5, 3, 2)
    return y.reshape(g * SEQ_PER_GROUP, l, n_heads * HEAD_SIZE)


def _state_to_scan(s, n_heads):
    b = s.shape[0]
    g = b // SEQ_PER_GROUP
    s = s.reshape(g, SEQ_PER_GROUP, n_heads, 2, HEAD_SIZE // 2, HEAD_SIZE).transpose(0, 5, 4, 3, 1, 2)
    return s.reshape(g, HEAD_SIZE, HEAD_SIZE // 2, 2 * SEQ_PER_GROUP * n_heads)


def _state_from_scan(s, n_heads):
    g = s.shape[0]
    s = s.reshape(g, HEAD_SIZE, HEAD_SIZE // 2, 2, SEQ_PER_GROUP, n_heads).transpose(0, 4, 5, 3, 2, 1)
    return s.reshape(g * SEQ_PER_GROUP, n_heads, HEAD_SIZE, HEAD_SIZE)


def _key_const(p, n_heads):
    c = p.reshape(n_heads, HEAD_SIZE).T
    return jnp.tile(c, (1, 2 * SEQ_PER_GROUP))


def _val_const(p, n_heads):
    c = p.reshape(n_heads, 2, HEAD_SIZE // 2).transpose(2, 1, 0)
    c = jnp.broadcast_to(c[:, :, None, :], (HEAD_SIZE // 2, 2, SEQ_PER_GROUP, n_heads))
    return c.reshape(HEAD_SIZE // 2, 2 * SEQ_PER_GROUP * n_heads)


def _pool_kernel(u_ref, up_ref, wp_ref, sc_ref, o_ref, ext_scr, *, lb, pos0, grp):
    ti = pl.program_id(1)
    nb = u_ref.shape[0]
    hist = POOL_BUF + 1

    @pl.when(ti == 0)
    def _():
        ext_scr[:, 0:1, :] = jnp.zeros((nb, 1, u_ref.shape[2]), F32)
        ext_scr[:, 1:hist, :] = up_ref[...]

    @pl.when(ti > 0)
    def _():
        ext_scr[:, 0:hist, :] = ext_scr[:, lb:lb + hist, :]

    u = u_ref[...]
    ext_scr[:, hist:hist + lb, :] = u
    pos = pos0 + ti * lb + lax.broadcasted_iota(jnp.int32, (1, lb, 1), 1)
    for j, win in enumerate(POOL_WINDOWS):
        c0, c1 = j * grp, (j + 1) * grp
        acc = u[:, :, c0:c1]
        for dlt in range(1, win):
            acc = acc + ext_scr[:, hist - dlt:hist - dlt + lb, c0:c1]
        cnt = jnp.minimum(pos + 1, win).astype(F32)
        pooled = acc / cnt - u[:, :, c0:c1]
        mixed = _dot(pooled.reshape(nb * lb, grp).astype(BF16), wp_ref[j])
        o_ref[:, :, c0:c1] = mixed.reshape(nb, lb, grp) * sc_ref[:, c0:c1]


def _pool(u, u_prev, w_pool_bf, pool_scale, pos0):
    b, l, d_b = u.shape
    grp = d_b // len(POOL_WINDOWS)
    lb = _pick(l, 512)
    nb = _pick(b, max(1, 128 // lb)) if lb < 128 else 1
    blk = pl.BlockSpec((nb, lb, d_b), lambda bi, ti: (bi, ti, 0))
    return pl.pallas_call(
        functools.partial(_pool_kernel, lb=lb, pos0=pos0, grp=grp),
        out_shape=jax.ShapeDtypeStruct(u.shape, F32),
        grid=(b // nb, l // lb),
        in_specs=[blk,
                  pl.BlockSpec((nb, POOL_BUF, d_b), lambda bi, ti: (bi, 0, 0)),
                  pl.BlockSpec(w_pool_bf.shape, lambda bi, ti: (0, 0, 0)),
                  pl.BlockSpec((1, d_b), lambda bi, ti: (0, 0))],
        out_specs=blk,
        scratch_shapes=[pltpu.VMEM((nb, lb + POOL_BUF + 1, d_b), F32)],
        compiler_params=_cparams(("parallel", "arbitrary")),
        name="pool_mix",
    )(u, u_prev, w_pool_bf, pool_scale)


def _merge_kernel(yo_ref, g_ref, ob_ref, ga_ref, gb_ref, wa_ref, wb_ref, o_ref):
    o_a = (yo_ref[...] * g_ref[...]).astype(BF16)
    pa = _dot(o_a, wa_ref[...])
    pb = _dot(ob_ref[...].astype(BF16), wb_ref[...])
    o_ref[...] = (jax.nn.sigmoid(ga_ref[...]) * pa + jax.nn.sigmoid(gb_ref[...]) * pb).astype(BF16)


def _merge(yo, g, o_b, gates, wa_bf, wb_bf):
    t, d_a = yo.shape
    d = wa_bf.shape[1]
    bt = _pick(t, 512)
    row = lambda i: (i, 0)
    const = lambda i: (0, 0)
    return pl.pallas_call(
        _merge_kernel,
        out_shape=jax.ShapeDtypeStruct((t, d), BF16),
        grid=(t // bt,),
        in_specs=[pl.BlockSpec((bt, d_a), row), pl.BlockSpec((bt, d_a), row),
                  pl.BlockSpec((bt, o_b.shape[1]), row),
                  pl.BlockSpec((bt, d), lambda i: (i, 0)), pl.BlockSpec((bt, d), lambda i: (i, 1)),
                  pl.BlockSpec(wa_bf.shape, const), pl.BlockSpec(wb_bf.shape, const)],
        out_specs=pl.BlockSpec((bt, d), row),
        compiler_params=_cparams(("parallel",)),
        name="branch_merge",
    )(yo, g, o_b, gates, gates, wa_bf, wb_bf)


def _outproj_router_kernel(x_ref, m_ref, wo_ref, nf_ref, wr_ref, br_ref,
                           x1_ref, h2_ref, idx_ref, gate_ref, *, n_exp):
    x1 = x_ref[...] + _dot(m_ref[...], wo_ref[...])
    x1_ref[...] = x1
    h2 = _rms(x1, nf_ref[...])
    h2_ref[...] = h2
    logits = _dot_hi(h2, wr_ref[...]) + br_ref[...]
    bt = logits.shape[0]
    lane = lax.broadcasted_iota(jnp.int32, (bt, n_exp), 1)
    out_lane = lax.broadcasted_iota(jnp.int32, (bt, LANES), 1)
    idx_out = jnp.zeros((bt, LANES), jnp.int32)
    val_out = jnp.zeros((bt, LANES), F32)
    vals = []
    for kx in range(TOP_K):
        mx = jnp.max(logits, axis=-1, keepdims=True)
        sel = jnp.min(jnp.where(logits == mx, lane, n_exp), axis=-1, keepdims=True)
        idx_out = jnp.where(out_lane == kx, sel, idx_out)
        vals.append(mx)
        logits = jnp.where(lane == sel, -jnp.inf, logits)
    exps = [jnp.exp(vx - vals[0]) for vx in vals]
    den = exps[0] + exps[1] + exps[2] + exps[3]
    for kx in range(TOP_K):
        val_out = jnp.where(out_lane == kx, exps[kx] / den, val_out)
    idx_ref[...] = idx_out
    gate_ref[...] = val_out


def _outproj_router(x, merged, wo_bf, norm_ffn, w_router, b_router):
    t, d = x.shape
    n_exp = w_router.shape[1]
    bt = _pick(t, 512)
    row = lambda i: (i, 0)
    const = lambda i: (0, 0)
    return pl.pallas_call(
        functools.partial(_outproj_router_kernel, n_exp=n_exp),
        out_shape=[jax.ShapeDtypeStruct((t, d), F32), jax.ShapeDtypeStruct((t, d), F32),
                   jax.ShapeDtypeStruct((t, LANES), jnp.int32), jax.ShapeDtypeStruct((t, LANES), F32)],
        grid=(t // bt,),
        in_specs=[pl.BlockSpec((bt, d), row), pl.BlockSpec((bt, d), row),
                  pl.BlockSpec((d, d), const), pl.BlockSpec((1, d), const),
                  pl.BlockSpec((d, n_exp), const), pl.BlockSpec((1, n_exp), const)],
        out_specs=[pl.BlockSpec((bt, d), row), pl.BlockSpec((bt, d), row),
                   pl.BlockSpec((bt, LANES), row), pl.BlockSpec((bt, LANES), row)],
        compiler_params=_cparams(("parallel",)),
        name="outproj_router",
    )(x, merged, wo_bf, norm_ffn, w_router, b_router)


def _row_copy(src_hbm, dst_vmem, sem, src_row, dst_row):
    return pltpu.make_async_copy(src_hbm.at[pl.ds(src_row, 1)], dst_vmem.at[pl.ds(dst_row, 1)], sem)


def _dispatch_kernel(rowtok_ref, x_hbm, o_ref, buf, sem, *, bm):
    base = pl.program_id(0) * bm

    def issue(r, c):
        _row_copy(x_hbm, buf, sem, rowtok_ref[base + r], r).start()
        return c

    lax.fori_loop(0, bm, issue, 0)

    def drain(r, c):
        _row_copy(x_hbm, buf, sem, 0, r).wait()
        return c

    lax.fori_loop(0, bm, drain, 0)
    o_ref[...] = buf[...].astype(BF16)


def _dispatch(row_tok, h2, n_rows, bm):
    d = h2.shape[1]
    return pl.pallas_call(
        functools.partial(_dispatch_kernel, bm=bm),
        out_shape=jax.ShapeDtypeStruct((n_rows, d), BF16),
        grid_spec=pltpu.PrefetchScalarGridSpec(
            num_scalar_prefetch=1,
            grid=(n_rows // bm,),
            in_specs=[pl.BlockSpec(memory_space=pl.ANY)],
            out_specs=pl.BlockSpec((bm, d), lambda i, rt: (i, 0)),
            scratch_shapes=[pltpu.VMEM((bm, d), F32), pltpu.SemaphoreType.DMA(())]),
        compiler_params=_cparams(("arbitrary",)),
        name="moe_dispatch",
    )(row_tok, h2)


def _expert_in_kernel(be_ref, bv_ref, x_ref, wg_ref, wu_ref, bg_ref, bu_ref, o_ref, wg_scr, wu_scr):
    i = pl.program_id(1)
    prev = be_ref[jnp.maximum(i - 1, 0)]
    fresh = jnp.logical_or(i == 0, be_ref[i] != prev)

    @pl.when(fresh)
    def _():
        wg_scr[...] = wg_ref[0].astype(BF16)
        wu_scr[...] = wu_ref[0].astype(BF16)

    @pl.when(bv_ref[i] > 0)
    def _():
        x = x_ref[...]
        gate = jnp.minimum(_dot(x, wg_scr[...]) + bg_ref[0], SWIGLU_LIMIT)
        up = jnp.clip(_dot(x, wu_scr[...]) + bu_ref[0], -SWIGLU_LIMIT, SWIGLU_LIMIT)
        glu = gate * jax.nn.sigmoid(gate * SWIGLU_ALPHA)
        o_ref[...] = ((up + 1.0) * glu).astype(BF16)

    @pl.when(bv_ref[i] == 0)
    def _():
        o_ref[...] = jnp.zeros(o_ref.shape, BF16)


def _expert_in(block_expert, block_valid, rows, w_exp_in, b_exp_in, bm):
    n_rows, d = rows.shape
    n_exp, _, two_ff = w_exp_in.shape
    d_ff = two_ff // 2
    tn = _pick(d_ff, 1024)
    nj = d_ff // tn
    b3 = b_exp_in.reshape(n_exp, 1, two_ff)
    return pl.pallas_call(
        _expert_in_kernel,
        out_shape=jax.ShapeDtypeStruct((n_rows, d_ff), BF16),
        grid_spec=pltpu.PrefetchScalarGridSpec(
            num_scalar_prefetch=2,
            grid=(nj, n_rows // bm),
            in_specs=[pl.BlockSpec((bm, d), lambda j, i, be, bv: (i, 0)),
                      pl.BlockSpec((1, d, tn), lambda j, i, be, bv: (be[i], 0, j)),
                      pl.BlockSpec((1, d, tn), lambda j, i, be, bv: (be[i], 0, nj + j)),
                      pl.BlockSpec((1, 1, tn), lambda j, i, be, bv: (be[i], 0, j)),
                      pl.BlockSpec((1, 1, tn), lambda j, i, be, bv: (be[i], 0, nj + j))],
            out_specs=pl.BlockSpec((bm, tn), lambda j, i, be, bv: (i, j)),
            scratch_shapes=[pltpu.VMEM((d, tn), BF16), pltpu.VMEM((d, tn), BF16)]),
        compiler_params=_cparams(("arbitrary", "arbitrary")),
        name="expert_in",
    )(block_expert, block_valid, rows, w_exp_in, w_exp_in, b3, b3)


def _expert_out_kernel(be_ref, bv_ref, h_ref, w_ref, b_ref, o_ref, w_scr):
    i = pl.program_id(0)
    prev = be_ref[jnp.maximum(i - 1, 0)]
    fresh = jnp.logical_or(i == 0, be_ref[i] != prev)

    @pl.when(fresh)
    def _():
        w_scr[...] = w_ref[0].astype(BF16)

    @pl.when(bv_ref[i] > 0)
    def _():
        o_ref[...] = _dot(h_ref[...], w_scr[...]) + b_ref[0]

    @pl.when(bv_ref[i] == 0)
    def _():
        o_ref[...] = jnp.zeros(o_ref.shape, F32)


def _expert_out(block_expert, block_valid, hid, w_exp_out, b_exp_out, bm):
    n_rows, d_ff = hid.shape
    n_exp, _, d = w_exp_out.shape
    b3 = b_exp_out.reshape(n_exp, 1, d)
    return pl.pallas_call(
        _expert_out_kernel,
        out_shape=jax.ShapeDtypeStruct((n_rows, d), F32),
        grid_spec=pltpu.PrefetchScalarGridSpec(
            num_scalar_prefetch=2,
            grid=(n_rows // bm,),
            in_specs=[pl.BlockSpec((bm, d_ff), lambda i, be, bv: (i, 0)),
                      pl.BlockSpec((1, d_ff, d), lambda i, be, bv: (be[i], 0, 0)),
                      pl.BlockSpec((1, 1, d), lambda i, be, bv: (be[i], 0, 0))],
            out_specs=pl.BlockSpec((bm, d), lambda i, be, bv: (i, 0)),
            scratch_shapes=[pltpu.VMEM((d_ff, d), BF16)]),
        compiler_params=_cparams(("arbitrary",)),
        name="expert_out",
    )(block_expert, block_valid, hid, w_exp_out, b3)


def _combine_ple_kernel(pos_ref, x1_ref, gate_ref, p_ref, np_ref, wg_ref, wp_ref, nfin_ref, eo_hbm,
                        o_ref, buf, sem, *, bt):
    base = pl.program_id(0) * bt

    def issue(r, c):
        for kx in range(TOP_K):
            _row_copy(eo_hbm, buf.at[kx], sem, pos_ref[(base + r) * TOP_K + kx], r).start()
        return c

    lax.fori_loop(0, bt, issue, 0)

    def drain(r, c):
        for kx in range(TOP_K):
            _row_copy(eo_hbm, buf.at[kx], sem, 0, r).wait()
        return c

    lax.fori_loop(0, bt, drain, 0)
    gates = gate_ref[...]
    x2 = x1_ref[...]
    for kx in range(TOP_K):
        x2 = x2 + gates[:, kx:kx + 1] * buf[kx]
    hp = _rms(x2, np_ref[...]).astype(BF16)
    x3 = x2 + jax.nn.sigmoid(_dot(hp, wg_ref[...])) * _dot(p_ref[...].astype(BF16), wp_ref[...])
    o_ref[...] = _rms(x3, nfin_ref[...])


def _combine_ple(pos, x1, gates, p, norm_ple, wg_bf, wp_bf, norm_final, eo):
    t, d = x1.shape
    d_ple = p.shape[1]
    bt = _pick(t, 256)
    row = lambda i, ps: (i, 0)
    const = lambda i, ps: (0, 0)
    return pl.pallas_call(
        functools.partial(_combine_ple_kernel, bt=bt),
        out_shape=jax.ShapeDtypeStruct((t, d), F32),
        grid_spec=pltpu.PrefetchScalarGridSpec(
            num_scalar_prefetch=1,
            grid=(t // bt,),
            in_specs=[pl.BlockSpec((bt, d), row), pl.BlockSpec((bt, LANES), row),
                      pl.BlockSpec((bt, d_ple), row), pl.BlockSpec((1, d), const),
                      pl.BlockSpec((d, d), const), pl.BlockSpec((d_ple, d), const),
                      pl.BlockSpec((1, d), const), pl.BlockSpec(memory_space=pl.ANY)],
            out_specs=pl.BlockSpec((bt, d), row),
            scratch_shapes=[pltpu.VMEM((TOP_K, bt, d), F32), pltpu.SemaphoreType.DMA(())]),
        compiler_params=_cparams(("arbitrary",)),
        name="combine_ple",
    )(pos, x1, gates, p, norm_ple, wg_bf, wp_bf, norm_final, eo)


def _routing(top_idx, n_exp, bm):
    t = top_idx.shape[0]
    n_assign = t * TOP_K
    e_flat = top_idx.reshape(-1)
    onehot = (e_flat[:, None] == jnp.arange(n_exp, dtype=jnp.int32)[None, :]).astype(jnp.int32)
    csum = jnp.cumsum(onehot, axis=0)
    rank = jnp.take_along_axis(csum, e_flat[:, None], axis=1)[:, 0] - 1
    counts = csum[-1]
    padded = (counts + bm - 1) // bm * bm
    pad_end = jnp.cumsum(padded)
    pad_start = pad_end - padded
    pos = (pad_start[e_flat] + rank).astype(jnp.int32)
    n_blocks = -(-n_assign // bm) + n_exp
    n_rows = n_blocks * bm
    row_tok = jnp.zeros((n_rows,), jnp.int32).at[pos].set(jnp.arange(n_assign, dtype=jnp.int32) // TOP_K)
    block_off = jnp.arange(n_blocks, dtype=jnp.int32) * bm
    block_expert = jnp.minimum(jnp.searchsorted(pad_end, block_off, side='right'), n_exp - 1).astype(jnp.int32)
    block_valid = (block_off < pad_end[-1]).astype(jnp.int32)
    return pos, row_tok, block_expert, block_valid, n_rows


def kernel(x_prompt, x_sample, state_wkv, state_shift, state_pool, p_prompt, p_sample, norm_mix, w_in, tm_mix, w0, w2, a0, a2, g2, k_k, k_a, r_k, ln_x_w, ln_x_b, w_pool, pool_scale, w_out_a, w_out_b, w_out, norm_ffn, w_router, b_router, w_exp_in, b_exp_in, w_exp_out, b_exp_out, norm_ple, w_ple_gate, w_ple_proj, norm_final):
    depth = w_in.shape[0]
    assert depth == 1, "single-layer step"
    bp, lp, d = x_prompt.shape
    bs, ls, _ = x_sample.shape
    d_a = w0.shape[1]
    d_b = pool_scale.shape[1]
    n_heads = d_a // HEAD_SIZE
    lw, la, lg = w2.shape[1], a2.shape[1], g2.shape[1]
    n_exp = w_router.shape[2]
    assert SEQ_PER_GROUP * n_heads * 2 == LANES and bp % SEQ_PER_GROUP == 0 and bs % SEQ_PER_GROUP == 0
    tp, ts = bp * lp, bs * ls
    t = tp + ts

    wi = w_in[0]
    o3 = 3 * d_a
    lwp, lap = -(-lw // LANES) * LANES, -(-la // LANES) * LANES
    zc = lambda n: jnp.zeros((d, n), F32)
    w_shift = jnp.concatenate([wi[:, :o3], wi[:, o3:o3 + lw], zc(lwp - lw), wi[:, o3 + lw:o3 + lw + la],
                               zc(lap - la), wi[:, o3 + lw + la:o3 + lw + la + lg]], axis=1).astype(BF16)
    n_shift = o3 + lw + la + lg
    w_u = wi[:, n_shift:n_shift + d_b].astype(BF16)
    w_g = wi[:, n_shift + d_b:].astype(BF16)
    tm = tm_mix[0]
    zt = lambda n: jnp.zeros((n,), F32)
    tm_p = jnp.concatenate([tm[:o3], tm[o3:o3 + lw], zt(lwp - lw), tm[o3 + lw:o3 + lw + la], zt(lap - la),
                            tm[o3 + lw + la:]])[None, :]
    w2p = jnp.concatenate([w2[0], jnp.zeros((lwp - lw, d_a), F32)], axis=0)
    a2p = jnp.concatenate([a2[0], jnp.zeros((lap - la, d_a), F32)], axis=0)

    x_all = jnp.concatenate([x_prompt.reshape(tp, d), x_sample.reshape(ts, d)], axis=0)
    p_all = jnp.concatenate([p_prompt[0].reshape(tp, -1), p_sample[0].reshape(ts, -1)], axis=0)
    nm = norm_mix[0][None, :]

    cur = _proj(x_all, nm, w_shift, normalize=True)
    u_all = _proj(x_all, nm, w_u, normalize=True)
    gates_ab = _proj(x_all, nm, w_g, normalize=True)
    first_prev_s = _proj(state_shift[0], nm, w_shift, normalize=False)
    h_last = _rmsnorm_rows(jnp.concatenate([x_prompt[:, -1], x_sample[:, -1]], axis=0), nm)

    nsp = cur.shape[1]
    cur_p = cur[:tp].reshape(bp, lp, nsp)
    cur_s = cur[tp:].reshape(bs, ls, nsp)
    prev_p = jnp.concatenate([jnp.zeros((bp, 1, nsp), F32), cur_p[:, :-1]], axis=1)
    prev_s = jnp.concatenate([first_prev_s[:, None], cur_s[:, :-1]], axis=1)
    prev = jnp.concatenate([prev_p.reshape(tp, nsp), prev_s.reshape(ts, nsp)], axis=0)

    r, k, v, a, w, g = _prep(cur, prev, tm_p, w0, a0, w2p, a2p, g2[0], d_a=d_a, lw=lwp, la=lap)

    kkc, kac, rkc = (_key_const(c, n_heads) for c in (k_k[0], k_a[0], r_k[0].reshape(-1)))
    lnw, lnb = _val_const(ln_x_w[0], n_heads), _val_const(ln_x_b[0], n_heads)

    def run_scan(lo, hi, b, l, s0):
        sl = lambda z: _to_scan_val(z[lo:hi].reshape(b, l, d_a), n_heads)
        yo, st = _scan(sl(r), sl(k), sl(v), sl(a), sl(w), _state_to_scan(s0, n_heads), kkc, kac, rkc, lnw, lnb)
        return _from_scan_val(yo, n_heads).reshape(b * l, d_a), _state_from_scan(st, n_heads)

    yo_p, wkv_p = run_scan(0, tp, bp, lp, jnp.zeros((bp, n_heads, HEAD_SIZE, HEAD_SIZE), F32))
    yo_s, wkv_s = run_scan(tp, t, bs, ls, state_wkv[0])
    yo = jnp.concatenate([yo_p, yo_s], axis=0)

    wp_bf = w_pool[0].astype(BF16)
    u_p = u_all[:tp].reshape(bp, lp, d_b)
    u_s = u_all[tp:].reshape(bs, ls, d_b)
    ob_p = _pool(u_p, jnp.zeros((bp, POOL_BUF, d_b), F32), wp_bf, pool_scale, 0)
    ob_s = _pool(u_s, state_pool[0], wp_bf, pool_scale, PAST_LEN)
    o_b = jnp.concatenate([ob_p.reshape(tp, d_b), ob_s.reshape(ts, d_b)], axis=0)
    pool_p = jnp.concatenate([jnp.zeros((bp, POOL_BUF, d_b), F32), u_p], axis=1)[:, -POOL_BUF:]
    pool_s = jnp.concatenate([state_pool[0], u_s], axis=1)[:, -POOL_BUF:]

    merged = _merge(yo, g, o_b, gates_ab, w_out_a[0].astype(BF16), w_out_b[0].astype(BF16))
    x1, h2, top_idx, gate_vals = _outproj_router(x_all, merged, w_out[0].astype(BF16), norm_ffn,
                                                 w_router[0], b_router)

    pos, row_tok, block_expert, block_valid, n_rows = _routing(top_idx[:, :TOP_K], n_exp, MOE_BM)
    rows = _dispatch(row_tok, h2, n_rows, MOE_BM)
    hid = _expert_in(block_expert, block_valid, rows, w_exp_in[0], b_exp_in[0], MOE_BM)
    eo = _expert_out(block_expert, block_valid, hid, w_exp_out[0], b_exp_out[0], MOE_BM)
    y = _combine_ple(pos, x1, gate_vals, p_all, norm_ple, w_ple_gate[0].astype(BF16),
                     w_ple_proj[0].astype(BF16), norm_final[None, :], eo)

    return (y[:tp].reshape(bp, lp, d), y[tp:].reshape(bs, ls, d),
            wkv_p[None], h_last[:bp][None], pool_p[None],
            wkv_s[None], h_last[bp:][None], pool_s[None])
```
